```python
import jax, jax.numpy as jnp
from jax import lax
import numpy as np

D_MODEL = 1024
BATCH = 8
SEQ = 2048
DEPTH = 2
DEC_BATCH = 128
DEC_SEQ = 8
PAST_LEN = 16384
PAGE_SIZE = 128

N_META = 16
D_CONV = D_MODEL
D_POOL = D_MODEL
CONV_WIDTH = 31
POOL_WINDOWS = (2, 4, 8, 16)
N_POOL_GROUPS = len(POOL_WINDOWS)
POOL_GROUP = D_POOL // N_POOL_GROUPS
POOL_MAX = max(POOL_WINDOWS)
CONV_BUF = CONV_WIDTH - 1
POOL_BUF = POOL_MAX - 1
RMS_EPS = 1e-6
LN_EPS = 1e-5
SPLITS = (D_CONV, D_CONV, D_CONV, D_POOL, D_POOL, D_MODEL, D_MODEL)
D_IN = sum(SPLITS)

kernel_name = "gated_conformer_conv_multiscale_pool_decoder_step"


def rmsnorm(x, g):
    xf = x.astype(jnp.float32)
    y = xf * lax.rsqrt(jnp.mean(xf * xf, axis=-1, keepdims=True) + RMS_EPS)
    return (y * g.astype(jnp.float32)).astype(x.dtype)


def layernorm(x, g, b):
    xf = x.astype(jnp.float32)
    mu = jnp.mean(xf, axis=-1, keepdims=True)
    var = jnp.mean(jnp.square(xf - mu), axis=-1, keepdims=True)
    y = (xf - mu) * lax.rsqrt(var + LN_EPS)
    return (y * g.astype(jnp.float32) + b.astype(jnp.float32)).astype(x.dtype)


def causal_depthwise_conv(u, buf, w, b):
    ext = jnp.concatenate([buf.astype(u.dtype), u], axis=1)
    y = lax.conv_general_dilated(ext, w[:, None, :].astype(u.dtype), window_strides=(1,), padding='VALID',
                                 dimension_numbers=('NWC', 'WIO', 'NWC'), feature_group_count=u.shape[-1])
    return y + b.astype(u.dtype), ext[:, -CONV_BUF:]


def multiscale_pool(u, buf, start_pos):
    T = u.shape[1]
    ext = jnp.concatenate([buf.astype(u.dtype), u], axis=1)
    cs = jnp.cumsum(ext.astype(jnp.float32), axis=1)
    cs = jnp.pad(cs, ((0, 0), (1, 0), (0, 0)))
    hi = cs[:, POOL_BUF + 1:POOL_BUF + 1 + T]
    pos = start_pos + jnp.arange(T)
    outs = []
    for g, w in enumerate(POOL_WINDOWS):
        sl = slice(g * POOL_GROUP, (g + 1) * POOL_GROUP)
        lo = cs[:, POOL_BUF + 1 - w:POOL_BUF + 1 - w + T, sl]
        cnt = jnp.minimum(w, pos + 1).astype(jnp.float32)[None, :, None]
        outs.append((hi[..., sl] - lo) / cnt)
    mean = jnp.concatenate(outs, axis=-1)
    return (mean - u.astype(jnp.float32)).astype(u.dtype), ext[:, -POOL_BUF:]


def mixer_layer(x, buf_conv, buf_pool, start_pos, norm_g, w_in, conv_w, conv_b, ln_g, ln_b,
                w_conv_out, w_pool_mix, pool_scale, w_pool_out, w_out):
    h = rmsnorm(x, norm_g)
    z = jnp.einsum('btd,de->bte', h, w_in)
    idx = np.cumsum(SPLITS)[:-1].tolist()
    a_val, a_gate, a_silu, p_in, p_silu, g_a, g_b = jnp.split(z, idx, axis=-1)
    u = a_val * jax.nn.sigmoid(a_gate)
    c, new_conv = causal_depthwise_conv(u, buf_conv, conv_w, conv_b)
    c = jax.nn.silu(layernorm(c, ln_g, ln_b)) * jax.nn.silu(a_silu)
    br_a = jnp.einsum('btc,cd->btd', c, w_conv_out)
    pooled, new_pool = multiscale_pool(p_in, buf_pool, start_pos)
    B_, T_ = pooled.shape[:2]
    pg = pooled.reshape(B_, T_, N_POOL_GROUPS, POOL_GROUP)
    q = jnp.einsum('btgc,gce->btge', pg, w_pool_mix).reshape(B_, T_, D_POOL)
    q = q * pool_scale * jax.nn.silu(p_silu)
    br_b = jnp.einsum('btc,cd->btd', q, w_pool_out)
    merged = jax.nn.sigmoid(g_a) * br_a + jax.nn.sigmoid(g_b) * br_b
    return x + jnp.einsum('btd,de->bte', merged, w_out), new_conv, new_pool


def run_trunk(x, bufs_conv, bufs_pool, start_pos, norm_g, w_in, conv_w, conv_b, ln_g, ln_b,
              w_conv_out, w_pool_mix, pool_scale, w_pool_out, w_out, final_g):
    new_c, new_p = [], []
    for l in range(DEPTH):
        x, nc, npool = mixer_layer(x, bufs_conv[l], bufs_pool[l], start_pos, norm_g[l], w_in[l], conv_w[l],
                                   conv_b[l], ln_g[l], ln_b[l], w_conv_out[l], w_pool_mix[l], pool_scale[l],
                                   w_pool_out[l], w_out[l])
        new_c.append(nc)
        new_p.append(npool)
    return rmsnorm(x, final_g), jnp.stack(new_c), jnp.stack(new_p)


def setup_inputs(seed: int = 0) -> dict:
    key = jax.random.key(seed)
    ks = jax.random.split(key, 20)
    f32 = jnp.float32
    nrm = lambda k, s, sc: jax.random.normal(k, s, f32) * sc
    return {
        'x_prompt': nrm(ks[0], (BATCH, SEQ, D_MODEL), 1.0),
        'x_sample': nrm(ks[1], (DEC_BATCH, DEC_SEQ, D_MODEL), 1.0),
        'state_conv': nrm(ks[2], (DEPTH, DEC_BATCH, CONV_BUF, D_CONV), 0.5),
        'state_pool': nrm(ks[3], (DEPTH, DEC_BATCH, POOL_BUF, D_POOL), 1.0),
        'meta_tokens': nrm(ks[4], (N_META, D_MODEL), 1.0),
        'norm_g': 1.0 + nrm(ks[5], (DEPTH, D_MODEL), 0.1),
        'w_in': nrm(ks[6], (DEPTH, D_MODEL, D_IN), D_MODEL ** -0.5),
        'conv_w': nrm(ks[7], (DEPTH, CONV_WIDTH, D_CONV), CONV_WIDTH ** -0.5),
        'conv_b': nrm(ks[8], (DEPTH, D_CONV), 0.01),
        'ln_g': 1.0 + nrm(ks[9], (DEPTH, D_CONV), 0.1),
        'ln_b': nrm(ks[10], (DEPTH, D_CONV), 0.01),
        'w_conv_out': nrm(ks[11], (DEPTH, D_CONV, D_MODEL), D_CONV ** -0.5),
        'w_pool_mix': nrm(ks[12], (DEPTH, N_POOL_GROUPS, POOL_GROUP, POOL_GROUP), POOL_GROUP ** -0.5),
        'pool_scale': 1.0 + nrm(ks[13], (DEPTH, D_POOL), 0.1),
        'w_pool_out': nrm(ks[14], (DEPTH, D_POOL, D_MODEL), D_POOL ** -0.5),
        'w_out': nrm(ks[15], (DEPTH, D_MODEL, D_MODEL), D_MODEL ** -0.5),
        'final_g': 1.0 + nrm(ks[16], (D_MODEL,), 0.1),
    }


def reference(x_prompt, x_sample, state_conv, state_pool, meta_tokens, norm_g, w_in, conv_w, conv_b,
              ln_g, ln_b, w_conv_out, w_pool_mix, pool_scale, w_pool_out, w_out, final_g):
    B = x_prompt.shape[0]
    meta = jnp.broadcast_to(meta_tokens.astype(x_prompt.dtype)[None], (B, N_META, D_MODEL))
    xp = jnp.concatenate([meta, x_prompt], axis=1)
    zc = jnp.zeros((DEPTH, B, CONV_BUF, D_CONV), x_prompt.dtype)
    zp = jnp.zeros((DEPTH, B, POOL_BUF, D_POOL), x_prompt.dtype)
    yp, new_state_conv_prompt, new_state_pool_prompt = run_trunk(
        xp, zc, zp, 0, norm_g, w_in, conv_w, conv_b, ln_g, ln_b, w_conv_out, w_pool_mix,
        pool_scale, w_pool_out, w_out, final_g)
    y_prompt = yp[:, N_META:]
    y_sample, new_state_conv_sample, new_state_pool_sample = run_trunk(
        x_sample, state_conv, state_pool, PAST_LEN, norm_g, w_in, conv_w, conv_b, ln_g, ln_b,
        w_conv_out, w_pool_mix, pool_scale, w_pool_out, w_out, final_g)
    return (y_prompt, y_sample, new_state_conv_prompt, new_state_pool_prompt,
            new_state_conv_sample, new_state_pool_sample)
```

```python
import functools

import jax
import jax.numpy as jnp
from jax import lax
from jax.experimental import pallas as pl
from jax.experimental.pallas import tpu as pltpu

D = 1024
N_META = 16
CONV_WIDTH = 31
CONV_BUF = CONV_WIDTH - 1
POOL_WINDOWS = (2, 4, 8, 16)
POOL_MAX = max(POOL_WINDOWS)
POOL_BUF = POOL_MAX - 1
POOL_GROUP = D // len(POOL_WINDOWS)
RMS_EPS = 1e-6
LN_EPS = 1e-5

LANES = 128
SUBLANES = 8
NSLAB = D // LANES
CONV_PAD = 32
POOL_PAD = 16
SEQ_CONV_PITCH = 40
SEQ_POOL_PITCH = 24
VMEM_LIMIT = 60 * 1024 * 1024


def _lanes(j):
    return slice(j * LANES, (j + 1) * LANES)


def _rows(ref, j, start, n=SUBLANES):
    return ref[j, pl.ds(start, n, stride=1), :]


def _sigmoid(v):
    return 1.0 / (1.0 + jnp.exp(-v))


def _silu(v):
    return v * _sigmoid(v)


def _rmsnorm(x, g):
    ms = jnp.mean(x * x, axis=-1, keepdims=True)
    return x * lax.rsqrt(ms + RMS_EPS) * g


def _layernorm(c, g, b):
    mu = jnp.mean(c, axis=-1, keepdims=True)
    cc = c - mu
    var = jnp.mean(cc * cc, axis=-1, keepdims=True)
    return cc * lax.rsqrt(var + LN_EPS) * g + b


def _dot(a, w):
    return jnp.dot(a.astype(jnp.bfloat16), w, preferred_element_type=jnp.float32)


def _conv_taps(load, ws):
    acc0 = load(0) * ws[0]
    acc1 = load(1) * ws[1]
    for k in range(2, CONV_WIDTH, 2):
        acc0 = acc0 + load(k) * ws[k]
    for k in range(3, CONV_WIDTH, 2):
        acc1 = acc1 + load(k) * ws[k]
    return acc0 + acc1


def _conv_weights(cw_ref, j):
    return [jnp.broadcast_to(cw_ref[k:k + 1, _lanes(j)], (SUBLANES, LANES)) for k in range(CONV_WIDTH)]


def _layer_tail(x, h, wi_ref, c, pooled, cb_ref, lg_ref, lb_ref, wco_ref, wmix_ref, ps_ref, wpo_ref,
                wo_ref, fg_ref, final_norm):
    a_silu = _dot(h, wi_ref[:, 2 * D:3 * D])
    ca = _silu(_layernorm(c + cb_ref[...], lg_ref[...], lb_ref[...])) * _silu(a_silu)
    br_a = _dot(ca, wco_ref[...])
    q = jnp.concatenate(
        [_dot(pooled[:, g * POOL_GROUP:(g + 1) * POOL_GROUP], wmix_ref[g]) for g in range(len(POOL_WINDOWS))],
        axis=-1)
    p_silu = _dot(h, wi_ref[:, 4 * D:5 * D])
    q = q * ps_ref[...] * _silu(p_silu)
    br_b = _dot(q, wpo_ref[...])
    g_a = _dot(h, wi_ref[:, 5 * D:6 * D])
    g_b = _dot(h, wi_ref[:, 6 * D:7 * D])
    merged = _sigmoid(g_a) * br_a + _sigmoid(g_b) * br_b
    out = x + _dot(merged, wo_ref[...])
    if final_norm:
        out = _rmsnorm(out, fg_ref[...])
    return out


def _seq_kernel(x_ref, ic_ref, ip_ref, ng_ref, wi_ref, cw_ref, cb_ref, lg_ref, lb_ref, wco_ref, wmix_ref,
                ps_ref, wpo_ref, wo_ref, fg_ref,
                y_ref, nc_ref, np_ref,
                ext_ref, extp_ref, c_ref, pool_ref, *, tm, start_pos, final_norm):
    i = pl.program_id(1)
    chunks = tm // SUBLANES
    unroll = min(4, chunks)

    @pl.when(i == 0)
    def _():
        for j in range(NSLAB):
            ext_ref[j, 0:CONV_PAD, :] = ic_ref[0, :, _lanes(j)]
            extp_ref[j, 0:POOL_PAD, :] = ip_ref[0, :, _lanes(j)]

    x = x_ref[0]
    h = _rmsnorm(x, ng_ref[...]).astype(jnp.bfloat16)
    u = _dot(h, wi_ref[:, 0:D]) * _sigmoid(_dot(h, wi_ref[:, D:2 * D]))
    p_in = _dot(h, wi_ref[:, 3 * D:4 * D])
    for j in range(NSLAB):
        ext_ref[j, CONV_PAD:CONV_PAD + tm, :] = u[:, _lanes(j)]
        extp_ref[j, POOL_PAD:POOL_PAD + tm, :] = p_in[:, _lanes(j)]

    for j in range(NSLAB):
        ws = _conv_weights(cw_ref, j)

        def conv_body(it, carry, j=j, ws=ws):
            for g in range(unroll):
                r0 = pl.multiple_of((it * unroll + g) * SUBLANES, SUBLANES)
                c_ref[pl.ds(r0, SUBLANES), _lanes(j)] = _conv_taps(
                    lambda k: _rows(ext_ref, j, r0 + k + (CONV_PAD - CONV_BUF)), ws)
            return carry

        lax.fori_loop(0, chunks // unroll, conv_body, 0)

    for gi, w in enumerate(POOL_WINDOWS):
        for j in range(gi * 2, gi * 2 + 2):

            def pool_body(it, carry, j=j, w=w):
                for g in range(unroll):
                    r0 = pl.multiple_of((it * unroll + g) * SUBLANES, SUBLANES)
                    tok = extp_ref[j, pl.ds(r0 + POOL_PAD, SUBLANES), :]
                    s = tok
                    for d in range(1, w):
                        s = s + _rows(extp_ref, j, r0 + POOL_PAD - d)
                    if start_pos >= POOL_BUF:
                        mean = s * (1.0 / w)
                    else:
                        pos = start_pos + i * tm + r0 + lax.broadcasted_iota(jnp.int32, (SUBLANES, LANES), 0)
                        mean = s / jnp.minimum(w, pos + 1).astype(jnp.float32)
                    pool_ref[pl.ds(r0, SUBLANES), _lanes(j)] = mean - tok
                return carry

            lax.fori_loop(0, chunks // unroll, pool_body, 0)

    for j in range(NSLAB):
        tail_c = ext_ref[j, tm:tm + CONV_PAD, :]
        tail_p = extp_ref[j, tm:tm + POOL_PAD, :]
        ext_ref[j, 0:CONV_PAD, :] = tail_c
        extp_ref[j, 0:POOL_PAD, :] = tail_p
        nc_ref[0, :, _lanes(j)] = tail_c
        np_ref[0, :, _lanes(j)] = tail_p

    y_ref[0] = _layer_tail(x, h, wi_ref, c_ref[...], pool_ref[...], cb_ref, lg_ref, lb_ref, wco_ref, wmix_ref,
                           ps_ref, wpo_ref, wo_ref, fg_ref, final_norm)


def _sample_kernel(x_ref, sc_ref, sp_ref, ng_ref, wi_ref, cw_ref, cb_ref, lg_ref, lb_ref, wco_ref, wmix_ref,
                   ps_ref, wpo_ref, wo_ref, fg_ref,
                   y_ref, nc_ref, np_ref,
                   ext_ref, extp_ref, u_ref, p_ref, c_ref, pool_ref, *, nseq, tlen, final_norm):
    x = x_ref[...]
    h = _rmsnorm(x, ng_ref[...]).astype(jnp.bfloat16)
    u_ref[...] = _dot(h, wi_ref[:, 0:D]) * _sigmoid(_dot(h, wi_ref[:, D:2 * D]))
    p_ref[...] = _dot(h, wi_ref[:, 3 * D:4 * D])

    def fill_body(s, carry):
        cb = pl.multiple_of(s * SEQ_CONV_PITCH, SUBLANES)
        pb = pl.multiple_of(s * SEQ_POOL_PITCH, SUBLANES)
        t0 = pl.multiple_of(s * tlen, SUBLANES)
        for j in range(NSLAB):
            for r in range(0, 24, SUBLANES):
                ext_ref[j, pl.ds(cb + r, SUBLANES), :] = sc_ref[s, r:r + SUBLANES, _lanes(j)]
            ext_ref[j, pl.ds(cb + 24, CONV_BUF - 24), :] = sc_ref[s, 24:CONV_BUF, _lanes(j)]
            ext_ref[j, pl.ds(cb + CONV_BUF, tlen, stride=1), :] = u_ref[pl.ds(t0, tlen), _lanes(j)]
            extp_ref[j, pl.ds(pb, SUBLANES), :] = sp_ref[s, 0:SUBLANES, _lanes(j)]
            extp_ref[j, pl.ds(pb + SUBLANES, POOL_BUF - SUBLANES), :] = sp_ref[s, SUBLANES:POOL_BUF, _lanes(j)]
            extp_ref[j, pl.ds(pb + POOL_BUF, tlen, stride=1), :] = p_ref[pl.ds(t0, tlen), _lanes(j)]
        return carry

    lax.fori_loop(0, nseq, fill_body, 0)

    for j in range(NSLAB):
        ws = _conv_weights(cw_ref, j)

        def conv_body(s, carry, j=j, ws=ws):
            cb = s * SEQ_CONV_PITCH
            t0 = pl.multiple_of(s * tlen, SUBLANES)
            c_ref[pl.ds(t0, tlen), _lanes(j)] = _conv_taps(lambda k: _rows(ext_ref, j, cb + k), ws)
            return carry

        lax.fori_loop(0, nseq, conv_body, 0)

    for gi, w in enumerate(POOL_WINDOWS):
        for j in range(gi * 2, gi * 2 + 2):

            def pool_body(s, carry, j=j, w=w):
                pb = s * SEQ_POOL_PITCH
                t0 = pl.multiple_of(s * tlen, SUBLANES)
                tok = _rows(extp_ref, j, pb + POOL_BUF)
                acc = tok
                for d in range(1, w):
                    acc = acc + _rows(extp_ref, j, pb + POOL_BUF - d)
                pool_ref[pl.ds(t0, tlen), _lanes(j)] = acc * (1.0 / w) - tok
                return carry

            lax.fori_loop(0, nseq, pool_body, 0)

    def state_body(s, carry):
        cb = pl.multiple_of(s * SEQ_CONV_PITCH, SUBLANES)
        pb = pl.multiple_of(s * SEQ_POOL_PITCH, SUBLANES)
        for j in range(NSLAB):
            for r in range(0, 24, SUBLANES):
                nc_ref[s, r:r + SUBLANES, _lanes(j)] = ext_ref[j, pl.ds(cb + tlen + r, SUBLANES), :]
            nc_ref[s, 24:CONV_BUF, _lanes(j)] = ext_ref[j, pl.ds(cb + tlen + 24, CONV_BUF - 24), :]
            np_ref[s, 0:SUBLANES, _lanes(j)] = extp_ref[j, pl.ds(pb + tlen, SUBLANES), :]
            np_ref[s, SUBLANES:POOL_BUF, _lanes(j)] = extp_ref[j, pl.ds(pb + tlen + SUBLANES, POOL_BUF - SUBLANES), :]
        return carry

    lax.fori_loop(0, nseq, state_body, 0)

    y_ref[...] = _layer_tail(x, h, wi_ref, c_ref[...], pool_ref[...], cb_ref, lg_ref, lb_ref, wco_ref, wmix_ref,
                             ps_ref, wpo_ref, wo_ref, fg_ref, final_norm)


def _const_spec(shape, ngrid):
    zeros = (0,) * len(shape)
    if ngrid == 1:
        return pl.BlockSpec(shape, lambda a: zeros, pipeline_mode=pl.Buffered(1))
    return pl.BlockSpec(shape, lambda a, b: zeros, pipeline_mode=pl.Buffered(1))


def _weight_specs(ngrid):
    shapes = [(1, D), (D, 7 * D), (CONV_WIDTH, D), (1, D), (1, D), (1, D), (D, D),
              (len(POOL_WINDOWS), POOL_GROUP, POOL_GROUP), (1, D), (D, D), (D, D), (1, D)]
    return [_const_spec(s, ngrid) for s in shapes]


def _run_seq(x, init_conv, init_pool, weights, *, tm, start_pos, final_norm):
    nb, T, _ = x.shape
    nt = T // tm
    kern = functools.partial(_seq_kernel, tm=tm, start_pos=start_pos, final_norm=final_norm)
    return pl.pallas_call(
        kern,
        grid=(nb, nt),
        in_specs=[pl.BlockSpec((1, tm, D), lambda b, i: (b, i, 0)),
                  pl.BlockSpec((1, CONV_PAD, D), lambda b, i: (0, 0, 0)),
                  pl.BlockSpec((1, POOL_PAD, D), lambda b, i: (0, 0, 0))] + _weight_specs(2),
        out_specs=[pl.BlockSpec((1, tm, D), lambda b, i: (b, i, 0)),
                   pl.BlockSpec((1, CONV_PAD, D), lambda b, i: (b, 0, 0)),
                   pl.BlockSpec((1, POOL_PAD, D), lambda b, i: (b, 0, 0))],
        out_shape=[jax.ShapeDtypeStruct((nb, T, D), jnp.float32),
                   jax.ShapeDtypeStruct((nb, CONV_PAD, D), jnp.float32),
                   jax.ShapeDtypeStruct((nb, POOL_PAD, D), jnp.float32)],
        scratch_shapes=[pltpu.VMEM((NSLAB, CONV_PAD + tm, LANES), jnp.float32),
                        pltpu.VMEM((NSLAB, POOL_PAD + tm, LANES), jnp.float32),
                        pltpu.VMEM((tm, D), jnp.float32),
                        pltpu.VMEM((tm, D), jnp.float32)],
        compiler_params=pltpu.CompilerParams(dimension_semantics=("arbitrary", "arbitrary"),
                                             vmem_limit_bytes=VMEM_LIMIT),
        name="seq_layer",
    )(x, init_conv, init_pool, *weights)


def _run_sample(x, sc, sp, weights, *, nseq, final_norm):
    B, tlen, _ = x.shape
    tm = nseq * tlen
    kern = functools.partial(_sample_kernel, nseq=nseq, tlen=tlen, final_norm=final_norm)
    y, nc, npool = pl.pallas_call(
        kern,
        grid=(B // nseq,),
        in_specs=[pl.BlockSpec((tm, D), lambda i: (i, 0)),
                  pl.BlockSpec((nseq, CONV_BUF, D), lambda i: (i, 0, 0)),
                  pl.BlockSpec((nseq, POOL_BUF, D), lambda i: (i, 0, 0))] + _weight_specs(1),
        out_specs=[pl.BlockSpec((tm, D), lambda i: (i, 0)),
                   pl.BlockSpec((nseq, CONV_BUF, D), lambda i: (i, 0, 0)),
                   pl.BlockSpec((nseq, POOL_BUF, D), lambda i: (i, 0, 0))],
        out_shape=[jax.ShapeDtypeStruct((B * tlen, D), jnp.float32),
                   jax.ShapeDtypeStruct((B, CONV_BUF, D), jnp.float32),
                   jax.ShapeDtypeStruct((B, POOL_BUF, D), jnp.float32)],
        scratch_shapes=[pltpu.VMEM((NSLAB, nseq * SEQ_CONV_PITCH, LANES), jnp.float32),
                        pltpu.VMEM((NSLAB, nseq * SEQ_POOL_PITCH, LANES), jnp.float32),
                        pltpu.VMEM((tm, D), jnp.float32),
                        pltpu.VMEM((tm, D), jnp.float32),
                        pltpu.VMEM((tm, D), jnp.float32),
                        pltpu.VMEM((tm, D), jnp.float32)],
        compiler_params=pltpu.CompilerParams(dimension_semantics=("arbitrary",),
                                             vmem_limit_bytes=VMEM_LIMIT),
        name="sample_layer",
    )(x.reshape(B * tlen, D), sc, sp, *weights)
    return y.reshape(B, tlen, D), nc, npool


def kernel(x_prompt, x_sample, state_conv, state_pool, meta_tokens, norm_g, w_in, conv_w, conv_b, ln_g, ln_b,
           w_conv_out, w_pool_mix, pool_scale, w_pool_out, w_out, final_g):
    depth = w_in.shape[0]
    bf = jnp.bfloat16
    row = lambda v: v.reshape(1, D)
    xm = meta_tokens[None]
    xp = x_prompt
    xs = x_sample
    ncp, npp, ncs, nps = [], [], [], []
    for l in range(depth):
        weights = (row(norm_g[l]), w_in[l].astype(bf), conv_w[l], row(conv_b[l]), row(ln_g[l]), row(ln_b[l]),
                   w_conv_out[l].astype(bf), w_pool_mix[l].astype(bf), row(pool_scale[l]),
                   w_pool_out[l].astype(bf), w_out[l].astype(bf), row(final_g))
        last = l == depth - 1
        zc = jnp.zeros((1, CONV_PAD, D), jnp.float32)
        zp = jnp.zeros((1, POOL_PAD, D), jnp.float32)
        xm, mc, mp = _run_seq(xm, zc, zp, weights, tm=N_META, start_pos=0, final_norm=last)
        xp, c_p, p_p = _run_seq(xp, mc, mp, weights, tm=512, start_pos=N_META, final_norm=last)
        xs, c_s, p_s = _run_sample(xs, state_conv[l], state_pool[l], weights, nseq=16, final_norm=last)
        ncp.append(c_p[:, CONV_PAD - CONV_BUF:])
        npp.append(p_p[:, POOL_PAD - POOL_BUF:])
        ncs.append(c_s)
        nps.append(p_s)
    return (xp, xs, jnp.stack(ncp), jnp.stack(npp), jnp.stack(ncs), jnp.stack(nps))
```

```python
import functools

import jax
import jax.numpy as jnp
from jax import lax
from jax.experimental import pallas as pl
from jax.experimental.pallas import tpu as pltpu

D = 1024
N_META = 16
CONV_WIDTH = 31
CONV_BUF = CONV_WIDTH - 1
POOL_WINDOWS = (2, 4, 8, 16)
POOL_MAX = max(POOL_WINDOWS)
POOL_BUF = POOL_MAX - 1
POOL_GROUP = D // len(POOL_WINDOWS)
RMS_EPS = 1e-6
LN_EPS = 1e-5

LANES = 128
SUBLANES = 8
NSLAB = D // LANES
N_IN_GROUPS = 7
G_AVAL, G_AGATE, G_ASILU, G_PIN, G_PSILU, G_GA, G_GB = range(N_IN_GROUPS)
CONV_PAD = 32
POOL_PAD = 16
CONV_ROWS_IN_FLIGHT = 8
PROMPT_TILE = 512
SAMPLE_SEQS = 16
VMEM_LIMIT = 60 * 1024 * 1024


def _lanes(j):
    return slice(j * LANES, (j + 1) * LANES)


def _rows(ref, j, start, n=SUBLANES):
    return ref[j, pl.ds(start, n, stride=1), :]


def _gate2(half_z):
    return 1.0 + jnp.tanh(half_z)


def _rmsnorm(x, g):
    ms = jnp.mean(x * x, axis=-1, keepdims=True)
    return x * lax.rsqrt(ms + RMS_EPS) * g


def _layernorm(c, g, b):
    mu = jnp.mean(c, axis=-1, keepdims=True)
    cc = c - mu
    var = jnp.mean(cc * cc, axis=-1, keepdims=True)
    return cc * lax.rsqrt(var + LN_EPS) * g + b


def _dot(a, w):
    return jnp.dot(a.astype(jnp.bfloat16), w, preferred_element_type=jnp.float32)


def _in_group(h, wi_ref, g):
    return _dot(h, wi_ref[:, g * D:(g + 1) * D])


def _slabs(ref):
    return jnp.concatenate([ref[j] for j in range(ref.shape[0])], axis=-1)


def _layer_tail(x, h, c, pooled, w):
    hp_silu = _in_group(h, w.wi, G_PSILU)
    q = jnp.concatenate(
        [_dot(pooled[:, g * POOL_GROUP:(g + 1) * POOL_GROUP], w.wmix[g]) for g in range(len(POOL_WINDOWS))],
        axis=-1)
    q = q * w.ps[...] * (hp_silu * _gate2(hp_silu))
    hbr_b = _dot(q, w.wpo[...])
    ha_silu = _in_group(h, w.wi, G_ASILU)
    hv = _layernorm(c + w.cb[...], w.lg[...], w.lb[...])
    ca = (hv * _gate2(hv)) * (ha_silu * _gate2(ha_silu))
    hbr_a = _dot(ca, w.wco[...])
    merged = _gate2(_in_group(h, w.wi, G_GA)) * hbr_a + _gate2(_in_group(h, w.wi, G_GB)) * hbr_b
    return x + _dot(merged, w.wo[...])


class _W:
    N = 12

    def __init__(self, refs):
        (self.ng, self.wi, self.cw, self.cb, self.lg, self.lb, self.wco, self.wmix, self.ps, self.wpo, self.wo,
         self.fg) = refs


def _seq_tile(x, w, nc_ref, np_ref, ext_ref, extp_ref, c_ref, pool_ref, *, tm, pos0):
    h = _rmsnorm(x, w.ng[...]).astype(jnp.bfloat16)

    p_in = _in_group(h, w.wi, G_PIN)
    for j in range(NSLAB):
        extp_ref[j, POOL_PAD:POOL_PAD + tm, :] = p_in[:, _lanes(j)]
    for gi, win in enumerate(POOL_WINDOWS):
        for j in range(gi * 2, gi * 2 + 2):
            for r0 in range(0, tm, SUBLANES):
                tok = extp_ref[j, r0 + POOL_PAD:r0 + POOL_PAD + SUBLANES, :]
                s = tok
                for d in range(1, win):
                    s = s + _rows(extp_ref, j, r0 + POOL_PAD - d)
                if pos0 is None:
                    mean = s * (1.0 / win)
                else:
                    pos = pos0 + r0 + lax.broadcasted_iota(jnp.int32, (SUBLANES, LANES), 0)
                    mean = s / jnp.minimum(win, pos + 1).astype(jnp.float32)
                pool_ref[j, r0:r0 + SUBLANES, :] = mean - tok

    u = _in_group(h, w.wi, G_AVAL) * _gate2(_in_group(h, w.wi, G_AGATE))
    for j in range(NSLAB):
        ext_ref[j, CONV_PAD:CONV_PAD + tm, :] = u[:, _lanes(j)]

    def conv_body(j, carry):
        ws = [jnp.broadcast_to(w.cw[j, k:k + 1, :], (SUBLANES, LANES)) for k in range(CONV_WIDTH)]
        for g0 in range(0, tm // SUBLANES, CONV_ROWS_IN_FLIGHT):
            units = range(g0, min(g0 + CONV_ROWS_IN_FLIGHT, tm // SUBLANES))
            accs = {g: None for g in units}
            for s in range(units[0] * SUBLANES, units[-1] * SUBLANES + CONV_WIDTH):
                window = _rows(ext_ref, j, s + (CONV_PAD - CONV_BUF))
                for g in units:
                    k = s - g * SUBLANES
                    if 0 <= k < CONV_WIDTH:
                        term = window * ws[k]
                        accs[g] = term if accs[g] is None else accs[g] + term
            for g in units:
                c_ref[j, g * SUBLANES:(g + 1) * SUBLANES, :] = accs[g]
        return carry

    lax.fori_loop(0, NSLAB, conv_body, 0)

    for j in range(NSLAB):
        tail_c = ext_ref[j, tm:tm + CONV_PAD, :]
        tail_p = extp_ref[j, tm:tm + POOL_PAD, :]
        ext_ref[j, 0:CONV_PAD, :] = tail_c
        extp_ref[j, 0:POOL_PAD, :] = tail_p
        nc_ref[:, _lanes(j)] = tail_c
        np_ref[:, _lanes(j)] = tail_p

    return _layer_tail(x, h, _slabs(c_ref), _slabs(pool_ref), w)


def _prompt_kernel(x_ref, ic_ref, ip_ref, *refs, tm, final_norm):
    w = _W(refs[:_W.N])
    y_ref, nc_ref, np_ref, ext_ref, extp_ref, c_ref, pool_ref = refs[_W.N:]

    @pl.when(pl.program_id(1) == 0)
    def _():
        for j in range(NSLAB):
            ext_ref[j, 0:CONV_PAD, :] = ic_ref[:, _lanes(j)]
            extp_ref[j, 0:POOL_PAD, :] = ip_ref[:, _lanes(j)]

    out = _seq_tile(x_ref[0], w, nc_ref.at[0], np_ref.at[0], ext_ref, extp_ref, c_ref, pool_ref, tm=tm, pos0=None)
    y_ref[0] = _rmsnorm(out, w.fg[...]) if final_norm else out


def _meta_kernel(x_ref, *refs, tm):
    w = _W(refs[:_W.N])
    nc_ref, np_ref, ext_ref, extp_ref, c_ref, pool_ref, xcar_ref = refs[_W.N:]

    @pl.when(pl.program_id(0) == 0)
    def _():
        xcar_ref[...] = x_ref[...]

    for j in range(NSLAB):
        ext_ref[j, 0:CONV_PAD, :] = jnp.zeros((CONV_PAD, LANES), jnp.float32)
        extp_ref[j, 0:POOL_PAD, :] = jnp.zeros((POOL_PAD, LANES), jnp.float32)
    xcar_ref[...] = _seq_tile(xcar_ref[...], w, nc_ref, np_ref, ext_ref, extp_ref, c_ref, pool_ref, tm=tm, pos0=0)


def _sample_kernel(x_ref, sc_ref, sp_ref, *refs, nseq, tlen, depth):
    w = _W(refs[:_W.N])
    y_ref, nc_ref, np_ref, u_ref, p_ref, c_ref, pool_ref, xs_ref = refs[_W.N:]
    layer, blk = pl.program_id(0), pl.program_id(1)
    tm = nseq * tlen

    @pl.when(layer == 0)
    def _():
        xs_ref[blk] = x_ref[...].reshape(tm, D)

    x = xs_ref[blk]
    h = _rmsnorm(x, w.ng[...]).astype(jnp.bfloat16)
    u_ref[...] = _in_group(h, w.wi, G_AVAL) * _gate2(_in_group(h, w.wi, G_AGATE))
    p_ref[...] = _in_group(h, w.wi, G_PIN)

    def conv_plane(s, rows, j):
        if s < CONV_BUF:
            return sc_ref[s, rows, _lanes(j)]
        return u_ref[(s - CONV_BUF) * nseq + rows.start:(s - CONV_BUF) * nseq + rows.stop, _lanes(j)]

    def pool_plane(s, rows, j):
        if s < POOL_BUF:
            return sp_ref[s, rows, _lanes(j)]
        return p_ref[(s - POOL_BUF) * nseq + rows.start:(s - POOL_BUF) * nseq + rows.stop, _lanes(j)]

    for j in range(NSLAB):
        ws = [jnp.broadcast_to(w.cw[j, k:k + 1, :], (SUBLANES, LANES)) for k in range(CONV_WIDTH)]
        for b0 in range(0, nseq, SUBLANES):
            rows = slice(b0, b0 + SUBLANES)
            accs = [None] * tlen
            for s in range(CONV_BUF + tlen):
                plane = conv_plane(s, rows, j)
                for t in range(tlen):
                    k = s - t
                    if 0 <= k < CONV_WIDTH:
                        term = plane * ws[k]
                        accs[t] = term if accs[t] is None else accs[t] + term
            for t in range(tlen):
                c_ref[t * nseq + b0:t * nseq + b0 + SUBLANES, _lanes(j)] = accs[t]

    for gi, win in enumerate(POOL_WINDOWS):
        for j in range(gi * 2, gi * 2 + 2):
            for b0 in range(0, nseq, SUBLANES):
                rows = slice(b0, b0 + SUBLANES)
                run = None
                for t in range(tlen):
                    tok = pool_plane(POOL_BUF + t, rows, j)
                    if run is None or win == 2:
                        run = tok
                        for d in range(1, win):
                            run = run + pool_plane(POOL_BUF + t - d, rows, j)
                    else:
                        run = run + tok - pool_plane(POOL_BUF + t - win, rows, j)
                    pool_ref[t * nseq + b0:t * nseq + b0 + SUBLANES, _lanes(j)] = run * (1.0 / win) - tok

    for s in range(CONV_BUF):
        src = s + tlen
        nc_ref[s] = sc_ref[src] if src < CONV_BUF else u_ref[(src - CONV_BUF) * nseq:(src - CONV_BUF + 1) * nseq, :]
    for s in range(POOL_BUF):
        src = s + tlen
        np_ref[s] = sp_ref[src] if src < POOL_BUF else p_ref[(src - POOL_BUF) * nseq:(src - POOL_BUF + 1) * nseq, :]

    out = _layer_tail(x, h, c_ref[...], pool_ref[...], w)
    xs_ref[blk] = out

    @pl.when(layer < depth - 1)
    def _():
        y_ref[...] = out.reshape(tlen, nseq, D)

    @pl.when(layer == depth - 1)
    def _():
        y_ref[...] = _rmsnorm(out, w.fg[...]).reshape(tlen, nseq, D)


def _weight_specs(layer_of):
    shapes = [(1, D), (D, N_IN_GROUPS * D), (NSLAB, CONV_WIDTH, LANES), (1, D), (1, D), (1, D), (D, D),
              (len(POOL_WINDOWS), POOL_GROUP, POOL_GROUP), (1, D), (D, D), (D, D)]
    specs = []
    for shape in shapes:
        zeros = (0,) * len(shape)
        specs.append(pl.BlockSpec((None,) + shape, lambda *g, zeros=zeros: (layer_of(*g),) + zeros,
                                  pipeline_mode=pl.Buffered(1)))
    specs.append(pl.BlockSpec((1, D), lambda *g: (0, 0), pipeline_mode=pl.Buffered(1)))
    return specs


def _seq_scratch(tm):
    return [pltpu.VMEM((NSLAB, CONV_PAD + tm, LANES), jnp.float32),
            pltpu.VMEM((NSLAB, POOL_PAD + tm, LANES), jnp.float32),
            pltpu.VMEM((NSLAB, tm, LANES), jnp.float32),
            pltpu.VMEM((NSLAB, tm, LANES), jnp.float32)]


def _run_meta(x, weights, depth):
    tm = x.shape[0]
    return pl.pallas_call(
        functools.partial(_meta_kernel, tm=tm),
        grid=(depth,),
        in_specs=[pl.BlockSpec((tm, D), lambda l: (0, 0))] + _weight_specs(lambda l: l),
        out_specs=[pl.BlockSpec((None, CONV_PAD, D), lambda l: (l, 0, 0)),
                   pl.BlockSpec((None, POOL_PAD, D), lambda l: (l, 0, 0))],
        out_shape=[jax.ShapeDtypeStruct((depth, CONV_PAD, D), jnp.float32),
                   jax.ShapeDtypeStruct((depth, POOL_PAD, D), jnp.float32)],
        scratch_shapes=_seq_scratch(tm) + [pltpu.VMEM((tm, D), jnp.float32)],
        compiler_params=pltpu.CompilerParams(dimension_semantics=("arbitrary",), vmem_limit_bytes=VMEM_LIMIT),
        name="meta_layers",
    )(x, *weights)


def _run_prompt(x, meta_conv, meta_pool, weights, layer, *, tm, final_norm):
    nb, T, _ = x.shape
    return pl.pallas_call(
        functools.partial(_prompt_kernel, tm=tm, final_norm=final_norm),
        grid=(nb, T // tm),
        in_specs=[pl.BlockSpec((1, tm, D), lambda b, i: (b, i, 0)),
                  pl.BlockSpec((None, CONV_PAD, D), lambda b, i: (layer, 0, 0)),
                  pl.BlockSpec((None, POOL_PAD, D), lambda b, i: (layer, 0, 0))]
        + _weight_specs(lambda b, i: layer),
        out_specs=[pl.BlockSpec((1, tm, D), lambda b, i: (b, i, 0)),
                   pl.BlockSpec((1, CONV_PAD, D), lambda b, i: (b, 0, 0)),
                   pl.BlockSpec((1, POOL_PAD, D), lambda b, i: (b, 0, 0))],
        out_shape=[jax.ShapeDtypeStruct((nb, T, D), jnp.float32),
                   jax.ShapeDtypeStruct((nb, CONV_PAD, D), jnp.float32),
                   jax.ShapeDtypeStruct((nb, POOL_PAD, D), jnp.float32)],
        scratch_shapes=_seq_scratch(tm),
        compiler_params=pltpu.CompilerParams(dimension_semantics=("arbitrary", "arbitrary"),
                                             vmem_limit_bytes=VMEM_LIMIT),
        name="prompt_layer",
    )(x, meta_conv, meta_pool, *weights)


def _run_sample(x_t, sc_t, sp_t, weights, *, nseq):
    tlen, B, _ = x_t.shape
    depth = sc_t.shape[0]
    tm = nseq * tlen
    return pl.pallas_call(
        functools.partial(_sample_kernel, nseq=nseq, tlen=tlen, depth=depth),
        grid=(depth, B // nseq),
        in_specs=[pl.BlockSpec((tlen, nseq, D), lambda l, i: (0, i, 0)),
                  pl.BlockSpec((None, CONV_BUF, nseq, D), lambda l, i: (l, 0, i, 0)),
                  pl.BlockSpec((None, POOL_BUF, nseq, D), lambda l, i: (l, 0, i, 0))]
        + _weight_specs(lambda l, i: l),
        out_specs=[pl.BlockSpec((tlen, nseq, D), lambda l, i: (0, i, 0)),
                   pl.BlockSpec((None, CONV_BUF, nseq, D), lambda l, i: (l, 0, i, 0)),
                   pl.BlockSpec((None, POOL_BUF, nseq, D), lambda l, i: (l, 0, i, 0))],
        out_shape=[jax.ShapeDtypeStruct((tlen, B, D), jnp.float32),
                   jax.ShapeDtypeStruct((depth, CONV_BUF, B, D), jnp.float32),
                   jax.ShapeDtypeStruct((depth, POOL_BUF, B, D), jnp.float32)],
        scratch_shapes=[pltpu.VMEM((tm, D), jnp.float32),
                        pltpu.VMEM((tm, D), jnp.float32),
                        pltpu.VMEM((tm, D), jnp.float32),
                        pltpu.VMEM((tm, D), jnp.float32),
                        pltpu.VMEM((B // nseq, tm, D), jnp.float32)],
        compiler_params=pltpu.CompilerParams(dimension_semantics=("arbitrary", "arbitrary"),
                                             vmem_limit_bytes=VMEM_LIMIT),
        name="sample_layers",
    )(x_t, sc_t, sp_t, *weights)


def _prepare_weights(norm_g, w_in, conv_w, conv_b, ln_g, ln_b, w_conv_out, w_pool_mix, pool_scale, w_pool_out,
                     w_out, final_g):
    depth = w_in.shape[0]
    bf = jnp.bfloat16
    rows = lambda v: v.reshape(depth, 1, D)
    in_scale = jnp.where(jnp.arange(N_IN_GROUPS * D) // D == G_PIN, 1.0, 0.5).astype(jnp.float32)
    wi = (w_in * in_scale).astype(bf)
    cw = conv_w.reshape(depth, CONV_WIDTH, NSLAB, LANES).transpose(0, 2, 1, 3)
    return (rows(norm_g), wi, cw, rows(conv_b), rows(0.5 * ln_g), rows(0.5 * ln_b),
            (0.5 * w_conv_out).astype(bf), w_pool_mix.astype(bf), rows(pool_scale),
            (0.5 * w_pool_out).astype(bf), w_out.astype(bf), final_g.reshape(1, D))


def kernel(x_prompt, x_sample, state_conv, state_pool, meta_tokens, norm_g, w_in, conv_w, conv_b, ln_g, ln_b,
           w_conv_out, w_pool_mix, pool_scale, w_pool_out, w_out, final_g):
    depth = w_in.shape[0]
    weights = _prepare_weights(norm_g, w_in, conv_w, conv_b, ln_g, ln_b, w_conv_out, w_pool_mix, pool_scale,
                               w_pool_out, w_out, final_g)
    meta_conv, meta_pool = _run_meta(meta_tokens, weights, depth)
    xp = x_prompt
    ncp, npp = [], []
    for layer in range(depth):
        xp, c_p, p_p = _run_prompt(xp, meta_conv, meta_pool, weights, layer, tm=PROMPT_TILE,
                                   final_norm=layer == depth - 1)
        ncp.append(c_p[:, CONV_PAD - CONV_BUF:])
        npp.append(p_p[:, POOL_PAD - POOL_BUF:])
    to_time_major = lambda s: s.transpose(0, 2, 1, 3)
    ys_t, ncs_t, nps_t = _run_sample(x_sample.transpose(1, 0, 2), to_time_major(state_conv),
                                     to_time_major(state_pool), weights, nseq=SAMPLE_SEQS)
    return (xp, ys_t.transpose(1, 0, 2), jnp.stack(ncp), jnp.stack(npp), to_time_major(ncs_t), to_time_major(nps_t))
```
